```python
import math
import jax, jax.numpy as jnp
from jax import lax
import numpy as np

D_MODEL = 1024
BATCH = 8
SEQ = 2048
DEPTH = 4
DEC_BATCH = 128
DEC_SEQ = 4
PAST_LEN = 2048
PAGE_SIZE = 128

HEAD_DIM = 64
FOX_HEADS = 8
FOX_W = FOX_HEADS * HEAD_DIM
FOX_FORGET_BIAS = 2.0
RG_W = 512
RG_BLOCKS = 8
RG_BW = RG_W // RG_BLOCKS
RG_CONV = 4
RG_C = 8.0
NSA_HEADS = 8
NSA_KV = 2
NSA_HPG = NSA_HEADS // NSA_KV
NSA_W = NSA_HEADS * HEAD_DIM
NSA_KVW = NSA_KV * HEAD_DIM
NSA_BLOCK = 64
NSA_TOPK = 8
NSA_WINDOW = 512
ROT_DIM = HEAD_DIM // 4
ROPE_THETA = 500000.0
D_FF = 4 * D_MODEL
Q_BLOCK = 128
NORM_EPS = 1e-6
NEG_INF = -1e30

IN_SPLITS = (FOX_W, FOX_W, FOX_W, FOX_HEADS,
             RG_W, RG_W,
             NSA_W, 6 * NSA_KVW, 3 * NSA_HEADS,
             3 * D_MODEL)
N_IN = int(sum(IN_SPLITS))
SPLIT_POINTS = [int(v) for v in np.cumsum(IN_SPLITS)[:-1]]
FOX_F_OFF = 3 * FOX_W

kernel_name = "fox_rglru_nsa_hybrid_decode_step"


def rms_norm(x, g):
    xf = x.astype(jnp.float32)
    y = xf * lax.rsqrt(jnp.mean(xf * xf, axis=-1, keepdims=True) + NORM_EPS)
    return (y * g.astype(jnp.float32)).astype(x.dtype)


def rotary(x, pos):
    half = ROT_DIM // 2
    inv = ROPE_THETA ** (-jnp.arange(half, dtype=jnp.float32) / half)
    ang = pos.astype(jnp.float32)[:, None] * inv[None, :]
    cos = jnp.cos(ang)[None, :, None, :]
    sin = jnp.sin(ang)[None, :, None, :]
    xr = x[..., :ROT_DIM].astype(jnp.float32)
    x1, x2 = xr[..., :half], xr[..., half:]
    rot = jnp.concatenate([x1 * cos - x2 * sin, x2 * cos + x1 * sin], axis=-1)
    return jnp.concatenate([rot.astype(x.dtype), x[..., ROT_DIM:]], axis=-1)


def sweep_queries(block_fn, tq):
    if tq <= Q_BLOCK or tq % Q_BLOCK:
        return block_fn(0, tq)
    out = lax.map(lambda i: block_fn(i * Q_BLOCK, Q_BLOCK), jnp.arange(tq // Q_BLOCK))
    out = jnp.moveaxis(out, 0, 1)
    return out.reshape(out.shape[0], tq, *out.shape[3:])


def fox_attention(q, k, v, F, q_pos):
    L = k.shape[1]
    scale = HEAD_DIM ** -0.5
    k_pos = jnp.arange(L)
    Ft = jnp.transpose(F, (0, 2, 1))

    def block(start, size):
        qb = lax.dynamic_slice_in_dim(q, start, size, 1)
        pb = lax.dynamic_slice_in_dim(q_pos, start, size, 0)
        Fq = jnp.take(Ft, pb, axis=2)
        s = jnp.einsum('bthd,bshd->bhts', qb, k, preferred_element_type=jnp.float32) * scale
        s = s + Fq[..., None] - Ft[:, :, None, :]
        s = jnp.where(k_pos[None, None, None, :] <= pb[None, None, :, None], s, NEG_INF)
        p = jax.nn.softmax(s, axis=-1)
        return jnp.einsum('bhts,bshd->bthd', p.astype(v.dtype), v)

    return sweep_queries(block, q.shape[1])


def nsa_attention(q, kc, vc, ks, vs, kw, vw, kw_pos, gates, pool, q_pos, banded):
    B, Tq = q.shape[0], q.shape[1]
    L = kc.shape[1]
    nb = -(-L // NSA_BLOCK)
    pad = nb * NSA_BLOCK - L
    scale = HEAD_DIM ** -0.5
    qg = q.reshape(B, Tq, NSA_KV, NSA_HPG, HEAD_DIM)
    g5 = gates.reshape(B, Tq, NSA_KV, NSA_HPG, 3)

    def to_blocks(a):
        return jnp.pad(a, ((0, 0), (0, pad), (0, 0), (0, 0))).reshape(B, nb, NSA_BLOCK, NSA_KV, HEAD_DIM)

    k_cmp = jnp.einsum('bnlgd,gl->bngd', to_blocks(kc), pool[0])
    v_cmp = jnp.einsum('bnlgd,gl->bngd', to_blocks(vc), pool[1])
    blk = jnp.arange(nb)
    cur = q_pos // NSA_BLOCK
    complete = ((blk[None, :] + 1) * NSA_BLOCK - 1) <= q_pos[:, None]
    s = jnp.einsum('btghd,bngd->bghtn', qg, k_cmp, preferred_element_type=jnp.float32) * scale
    s = jnp.where(complete, s, NEG_INF)
    any_c = jnp.any(complete, axis=-1).astype(jnp.float32)[None, None, None, :, None]
    p_cmp = jax.nn.softmax(s, axis=-1) * any_c
    o_cmp = jnp.einsum('bghtn,bngd->btghd', p_cmp.astype(vc.dtype), v_cmp)

    imp = jnp.sum(p_cmp, axis=2)
    forced = (blk[None, :] == cur[:, None]) | (blk[None, :] == 0)
    causal_blk = blk[None, :] <= cur[:, None]
    score = jnp.where(forced, 2.0 * NSA_HPG, jnp.where(causal_blk, imp, -1.0))
    n_sel = min(NSA_TOPK, nb)
    _, idx = lax.top_k(score, n_sel)
    ksb = jnp.transpose(to_blocks(ks), (0, 3, 1, 2, 4))
    vsb = jnp.transpose(to_blocks(vs), (0, 3, 1, 2, 4))
    gather = jax.vmap(jax.vmap(lambda a, i: a[i]))

    if banded:
        kw_src = jnp.pad(kw, ((0, 0), (NSA_WINDOW, 0), (0, 0), (0, 0)))
        vw_src = jnp.pad(vw, ((0, 0), (NSA_WINDOW, 0), (0, 0), (0, 0)))
        pos_src = jnp.concatenate([jnp.full((NSA_WINDOW,), -2 * NSA_WINDOW, dtype=kw_pos.dtype), kw_pos])
    else:
        kw_src, vw_src, pos_src = kw, vw, kw_pos

    def block(start, size):
        qb = lax.dynamic_slice_in_dim(qg, start, size, 1)
        pb = lax.dynamic_slice_in_dim(q_pos, start, size, 0)
        ib = lax.dynamic_slice_in_dim(idx, start, size, 2)
        gb = lax.dynamic_slice_in_dim(g5, start, size, 1)
        kg = gather(ksb, ib)
        vg = gather(vsb, ib)
        kpos = ib[..., None] * NSA_BLOCK + jnp.arange(NSA_BLOCK)
        ss = jnp.einsum('btghd,bgtkld->bghtkl', qb, kg, preferred_element_type=jnp.float32) * scale
        ss = jnp.where((kpos <= pb[None, None, :, None, None])[:, :, None], ss, NEG_INF)
        ps = jax.nn.softmax(ss.reshape(*ss.shape[:4], -1), axis=-1).reshape(ss.shape)
        o_sel = jnp.einsum('bghtkl,bgtkld->btghd', ps.astype(vs.dtype), vg)
        if banded:
            kwb = lax.dynamic_slice_in_dim(kw_src, start, size + NSA_WINDOW, 1)
            vwb = lax.dynamic_slice_in_dim(vw_src, start, size + NSA_WINDOW, 1)
            pwb = lax.dynamic_slice_in_dim(pos_src, start, size + NSA_WINDOW, 0)
        else:
            kwb, vwb, pwb = kw_src, vw_src, pos_src
        diff = pb[:, None] - pwb[None, :]
        m = (diff >= 0) & (diff < NSA_WINDOW)
        sw = jnp.einsum('btghd,bsgd->bghts', qb, kwb, preferred_element_type=jnp.float32) * scale
        pw = jax.nn.softmax(jnp.where(m, sw, NEG_INF), axis=-1)
        o_win = jnp.einsum('bghts,bsgd->btghd', pw.astype(vw.dtype), vwb)
        return gb[..., 1:2] * o_sel + gb[..., 2:3] * o_win

    o_local = sweep_queries(block, Tq)
    o = g5[..., 0:1] * o_cmp + o_local
    return o.reshape(B, Tq, NSA_W)


def rg_lru(xr, yg, conv_buf, h0, w_conv, b_conv, w_a, b_a, w_i, b_i, lam):
    B, T, _ = xr.shape
    xc = jnp.concatenate([conv_buf, xr], axis=1)
    conv = b_conv + sum(w_conv[i] * xc[:, i:i + T] for i in range(RG_CONV))
    new_buf = xc[:, xc.shape[1] - (RG_CONV - 1):]
    xb = conv.reshape(B, T, RG_BLOCKS, RG_BW)
    r = jax.nn.sigmoid(jnp.einsum('btni,nij->btnj', xb, w_a).reshape(B, T, RG_W) + b_a)
    ig = jax.nn.sigmoid(jnp.einsum('btni,nij->btnj', xb, w_i).reshape(B, T, RG_W) + b_i)
    log_a = (-RG_C * r.astype(jnp.float32)) * jax.nn.softplus(-lam.astype(jnp.float32))
    a = jnp.exp(log_a)
    u = jnp.sqrt(-jnp.expm1(2.0 * log_a)) * (ig * conv).astype(jnp.float32)

    def step(h, au):
        h = au[0] * h + au[1]
        return h, h

    hT, hs = lax.scan(step, h0.astype(jnp.float32), (jnp.swapaxes(a, 0, 1), jnp.swapaxes(u, 0, 1)))
    hs = jnp.swapaxes(hs, 0, 1).astype(xr.dtype)
    return hs * jax.nn.gelu(yg), hT.astype(xr.dtype), new_buf


def mixer_sublayer(u, l, P, past, pos0):
    B, T, _ = u.shape
    z = u @ P['w_in'][l] + P['b_in'][l]
    fq, fk, fv, ff, rx, ry, nq, nkv, ng, mg = jnp.split(z, SPLIT_POINTS, axis=-1)
    q_pos = pos0 + jnp.arange(T)

    fq = fq.reshape(B, T, FOX_HEADS, HEAD_DIM)
    fk = fk.reshape(B, T, FOX_HEADS, HEAD_DIM)
    fv = fv.reshape(B, T, FOX_HEADS, HEAD_DIM)
    logf = jax.nn.log_sigmoid(ff.astype(jnp.float32))
    if past is None:
        k_all, v_all, logf_all = fk, fv, logf
    else:
        k_all = jnp.concatenate([past['fox_kv'][:, :, 0], fk], axis=1)
        v_all = jnp.concatenate([past['fox_kv'][:, :, 1], fv], axis=1)
        logf_all = jnp.concatenate([past['fox_logf'].astype(jnp.float32), logf], axis=1)
    F = jnp.cumsum(logf_all, axis=1)
    o_fox = fox_attention(fq, k_all, v_all, F, q_pos).reshape(B, T, FOX_W)
    new_fox_kv = jnp.stack([fk, fv], axis=2)
    new_fox_logf = logf.astype(u.dtype)

    if past is None:
        conv_buf = jnp.zeros((B, RG_CONV - 1, RG_W), u.dtype)
        h0 = jnp.zeros((B, RG_W), u.dtype)
    else:
        conv_buf, h0 = past['rg_conv'], past['rg_h']
    o_rg, h_T, new_buf = rg_lru(rx, ry, conv_buf, h0, P['rg_w_conv'][l], P['rg_b_conv'][l],
                                P['rg_w_a'][l], P['rg_b_a'][l], P['rg_w_i'][l], P['rg_b_i'][l], P['rg_lam'][l])

    nq = rotary(nq.reshape(B, T, NSA_HEADS, HEAD_DIM), q_pos)
    nkv = nkv.reshape(B, T, 6, NSA_KV, HEAD_DIM)
    kc = rotary(nkv[:, :, 0], q_pos)
    ks = rotary(nkv[:, :, 2], q_pos)
    kw = rotary(nkv[:, :, 4], q_pos)
    new_nsa_kv = jnp.stack([kc, nkv[:, :, 1], ks, nkv[:, :, 3]], axis=2)
    new_win = jnp.stack([kw, nkv[:, :, 5]], axis=2)
    if past is None:
        full, win_all, win_pos, banded = new_nsa_kv, new_win, q_pos, True
        keep = min(NSA_WINDOW, T)
    else:
        full = jnp.concatenate([past['nsa_kv'], new_nsa_kv], axis=1)
        win_all = jnp.concatenate([past['win_kv'], new_win], axis=1)
        keep = past['win_kv'].shape[1]
        win_pos = pos0 - keep + jnp.arange(keep + T)
        banded = False
    new_win_state = win_all[:, win_all.shape[1] - keep:]
    gates = jax.nn.sigmoid(ng).reshape(B, T, NSA_HEADS, 3)
    o_nsa = nsa_attention(nq, full[:, :, 0], full[:, :, 1], full[:, :, 2], full[:, :, 3],
                          win_all[:, :, 0], win_all[:, :, 1], win_pos, gates, P['nsa_pool'][l], q_pos, banded)

    g_fox, g_rg, g_nsa = jnp.split(jax.nn.sigmoid(mg), 3, axis=-1)
    mixed = (g_fox * (o_fox @ P['w_br_fox'][l]) + g_rg * (o_rg @ P['w_br_rg'][l])
             + g_nsa * (o_nsa @ P['w_br_nsa'][l]))
    out = mixed @ P['w_out'][l]
    return out, (new_fox_kv, new_fox_logf, new_nsa_kv, new_win_state, h_T, new_buf)


def trunk(x, c, pos0, past_at, P):
    states = []
    cs = jax.nn.silu(c)
    for l in range(DEPTH):
        m = (cs @ P['w_ada'][l] + P['b_ada'][l])[:, None, :]
        sh1, sc1, gt1, sh2, sc2, gt2 = jnp.split(m, 6, axis=-1)
        u = rms_norm(x, P['norm_mix'][l]) * (1.0 + sc1) + sh1
        out, st = mixer_sublayer(u, l, P, past_at(l), pos0)
        x = x + gt1 * out
        u = rms_norm(x, P['norm_mlp'][l]) * (1.0 + sc2) + sh2
        hdn = jnp.square(jax.nn.relu(u @ P['w_up'][l]))
        x = x + gt2 * (hdn @ P['w_down'][l])
        states.append(st)
    y = rms_norm(x, P['norm_final'])
    stacked = [jnp.stack(list(s), axis=0) for s in zip(*states)]
    return y, stacked


def setup_inputs(seed: int = 0) -> dict:
    key = jax.random.key(seed)
    ks = iter(jax.random.split(key, 40))
    f32 = jnp.float32

    def nrm(shape, scale=1.0):
        return jax.random.normal(next(ks), shape, f32) * scale

    n_pages = PAST_LEN // PAGE_SIZE
    n_used = DEC_BATCH * n_pages
    n_phys = n_used + n_used // 4
    w_buf = min(NSA_WINDOW, PAST_LEN)
    perm = jax.random.permutation(next(ks), n_phys)[:n_used]
    page_table = perm.reshape(DEC_BATCH, n_pages).astype(jnp.int32)

    b_in = nrm((DEPTH, N_IN), 0.02)
    b_in = b_in.at[:, FOX_F_OFF:FOX_F_OFF + FOX_HEADS].add(FOX_FORGET_BIAS)
    a_init = jax.random.uniform(next(ks), (DEPTH, RG_W), f32, 0.9, 0.999)

    return {
        'x_prompt': nrm((BATCH, SEQ, D_MODEL)),
        'x_sample': nrm((DEC_BATCH, DEC_SEQ, D_MODEL)),
        'cache_fox_kv': nrm((DEPTH, n_phys, PAGE_SIZE, 2, FOX_HEADS, HEAD_DIM)),
        'cache_fox_logf': jax.nn.log_sigmoid(FOX_FORGET_BIAS + nrm((DEPTH, n_phys, PAGE_SIZE, FOX_HEADS))),
        'cache_nsa_kv': nrm((DEPTH, n_phys, PAGE_SIZE, 4, NSA_KV, HEAD_DIM)),
        'state_win_kv': nrm((DEPTH, DEC_BATCH, w_buf, 2, NSA_KV, HEAD_DIM)),
        'state_rg_h': nrm((DEPTH, DEC_BATCH, RG_W), 0.5),
        'state_rg_conv': nrm((DEPTH, DEC_BATCH, RG_CONV - 1, RG_W)),
        'page_table': page_table,
        'c_prompt': nrm((BATCH, D_MODEL)),
        'c_sample': nrm((DEC_BATCH, D_MODEL)),
        'norm_mix': 1.0 + nrm((DEPTH, D_MODEL), 0.02),
        'norm_mlp': 1.0 + nrm((DEPTH, D_MODEL), 0.02),
        'norm_final': 1.0 + nrm((D_MODEL,), 0.02),
        'w_ada': nrm((DEPTH, D_MODEL, 6 * D_MODEL), 0.5 * D_MODEL ** -0.5),
        'b_ada': nrm((DEPTH, 6 * D_MODEL), 0.02),
        'w_in': nrm((DEPTH, D_MODEL, N_IN), D_MODEL ** -0.5),
        'b_in': b_in,
        'rg_w_conv': nrm((DEPTH, RG_CONV, RG_W), RG_CONV ** -0.5),
        'rg_b_conv': nrm((DEPTH, RG_W), 0.02),
        'rg_w_a': nrm((DEPTH, RG_BLOCKS, RG_BW, RG_BW), RG_BW ** -0.5),
        'rg_b_a': nrm((DEPTH, RG_W), 0.02),
        'rg_w_i': nrm((DEPTH, RG_BLOCKS, RG_BW, RG_BW), RG_BW ** -0.5),
        'rg_b_i': nrm((DEPTH, RG_W), 0.02),
        'rg_lam': jnp.log(a_init) - jnp.log1p(-a_init),
        'nsa_pool': (1.0 + nrm((DEPTH, 2, NSA_KV, NSA_BLOCK), 0.1)) / NSA_BLOCK,
        'w_br_fox': nrm((DEPTH, FOX_W, D_MODEL), FOX_W ** -0.5),
        'w_br_rg': nrm((DEPTH, RG_W, D_MODEL), RG_W ** -0.5),
        'w_br_nsa': nrm((DEPTH, NSA_W, D_MODEL), NSA_W ** -0.5),
        'w_out': nrm((DEPTH, D_MODEL, D_MODEL), D_MODEL ** -0.5),
        'w_up': nrm((DEPTH, D_MODEL, D_FF), D_MODEL ** -0.5),
        'w_down': nrm((DEPTH, D_FF, D_MODEL), D_FF ** -0.5),
    }


def reference(x_prompt, x_sample, cache_fox_kv, cache_fox_logf, cache_nsa_kv, state_win_kv,
              state_rg_h, state_rg_conv, page_table, c_prompt, c_sample,
              norm_mix, norm_mlp, norm_final, w_ada, b_ada, w_in, b_in,
              rg_w_conv, rg_b_conv, rg_w_a, rg_b_a, rg_w_i, rg_b_i, rg_lam, nsa_pool,
              w_br_fox, w_br_rg, w_br_nsa, w_out, w_up, w_down):
    P = dict(norm_mix=norm_mix, norm_mlp=norm_mlp, norm_final=norm_final, w_ada=w_ada, b_ada=b_ada,
             w_in=w_in, b_in=b_in, rg_w_conv=rg_w_conv, rg_b_conv=rg_b_conv, rg_w_a=rg_w_a,
             rg_b_a=rg_b_a, rg_w_i=rg_w_i, rg_b_i=rg_b_i, rg_lam=rg_lam, nsa_pool=nsa_pool,
             w_br_fox=w_br_fox, w_br_rg=w_br_rg, w_br_nsa=w_br_nsa, w_out=w_out, w_up=w_up, w_down=w_down)
    n_pages = PAST_LEN // PAGE_SIZE
    db = x_sample.shape[0]

    y_prompt, p_states = trunk(x_prompt, c_prompt, 0, lambda l: None, P)

    def past_at(l):
        return {
            'fox_kv': cache_fox_kv[l][page_table].reshape(db, n_pages * PAGE_SIZE, 2, FOX_HEADS, HEAD_DIM),
            'fox_logf': cache_fox_logf[l][page_table].reshape(db, n_pages * PAGE_SIZE, FOX_HEADS),
            'nsa_kv': cache_nsa_kv[l][page_table].reshape(db, n_pages * PAGE_SIZE, 4, NSA_KV, HEAD_DIM),
            'win_kv': state_win_kv[l],
            'rg_h': state_rg_h[l],
            'rg_conv': state_rg_conv[l],
        }

    y_sample, s_states = trunk(x_sample, c_sample, PAST_LEN, past_at, P)
    p_fox_kv, p_fox_logf, p_nsa_kv, p_win_kv, p_rg_h, p_rg_conv = p_states
    s_fox_kv, s_fox_logf, s_nsa_kv, s_win_kv, s_rg_h, s_rg_conv = s_states
    return (y_prompt, y_sample, p_fox_kv, p_fox_logf, p_nsa_kv, p_win_kv, p_rg_h, p_rg_conv,
            s_fox_kv, s_fox_logf, s_nsa_kv, s_win_kv, s_rg_h, s_rg_conv)
```

```python
import functools
import numpy as np
import jax
import jax.numpy as jnp
from jax import lax
from jax.experimental import pallas as pl
from jax.experimental.pallas import tpu as pltpu

F32 = jnp.float32
BF16 = jnp.bfloat16

HEAD_DIM = 64
NSA_HPG = 4
NSA_BLOCK = 64
NSA_TOPK = 8
NSA_WINDOW = 512
ROT_HALF = 8
ROPE_THETA = 500000.0
RG_C = 8.0
NORM_EPS = 1e-6
NEG_INF = -1e30
Q_SCALE = HEAD_DIM ** -0.5

LANES = 128
VMEM_LIMIT = 56 * 1024 * 1024

C_FQ, C_FKV, C_RX, C_RY, C_NQ, C_NKV, C_MG, C_SM, C_END = 0, 512, 1536, 2048, 2560, 3072, 3840, 6912, 7040


def _params(sem):
    return pltpu.CompilerParams(dimension_semantics=sem, vmem_limit_bytes=VMEM_LIMIT)


def _sigmoid(x):
    return 1.0 / (1.0 + jnp.exp(-x))


def _dot(a, b):
    return jnp.dot(a, b, preferred_element_type=F32)


def _dot_nt(a, b):
    return lax.dot_general(a, b, (((1,), (1,)), ((), ())), preferred_element_type=F32)


def _rms_mod(x, g, sc, sh):
    ms = jnp.mean(x * x, axis=-1, keepdims=True)
    y = x * lax.rsqrt(ms + NORM_EPS) * g
    return y * (1.0 + sc) + sh


def _ada_kernel(c_ref, w_ref, b_ref, o_ref):
    c = c_ref[...]
    cs = c * _sigmoid(c)
    o_ref[0] = _dot(cs.astype(BF16), w_ref[0]) + b_ref[0]


def _ada(c_all, w_ada, b_ada):
    depth, d, n = w_ada.shape
    rows = c_all.shape[0]
    tn = 1536
    return pl.pallas_call(
        _ada_kernel,
        grid=(depth, n // tn),
        in_specs=[pl.BlockSpec((rows, d), lambda l, j: (0, 0)),
                  pl.BlockSpec((1, d, tn), lambda l, j: (l, 0, j)),
                  pl.BlockSpec((1, 1, tn), lambda l, j: (l, 0, j))],
        out_specs=pl.BlockSpec((1, rows, tn), lambda l, j: (l, 0, j)),
        out_shape=jax.ShapeDtypeStruct((depth, rows, n), F32),
        compiler_params=_params(("parallel", "parallel")),
        name="ada",
    )(c_all, w_ada, b_ada.reshape(depth, 1, n))


def _rope(x, c, s1, s2):
    return x * c + pltpu.roll(x, LANES - ROT_HALF, 1) * s1 + pltpu.roll(x, ROT_HALF, 1) * s2


def _inproj_kernel(x_ref, sc_ref, sh_ref, g_ref, w_ref, b_ref, c_ref, s1_ref, s2_ref,
                   fq_ref, fkv_ref, rx_ref, ry_ref, nq_ref, nsa_ref, win_ref, mg_ref, sm_ref):
    u = _rms_mod(x_ref[...], g_ref[...], sc_ref[0], sh_ref[0])
    ub = u.astype(BF16)

    def seg(a, b):
        return _dot(ub, w_ref[:, a:b]) + b_ref[:, a:b]

    c, s1, s2 = c_ref[...], s1_ref[...], s2_ref[...]

    fq_ref[...] = (seg(C_FQ, C_FKV) * Q_SCALE).astype(BF16)
    fkv_ref[...] = seg(C_FKV, C_RX)
    rx_ref[...] = seg(C_RX, C_RY)
    ry_ref[...] = seg(C_RY, C_NQ)

    nq = seg(C_NQ, C_NKV)
    for j in range(4):
        blk = _rope(nq[:, j * LANES:(j + 1) * LANES], c, s1, s2)
        nq_ref[:, j * LANES:(j + 1) * LANES] = (blk * Q_SCALE).astype(BF16)

    nkv = seg(C_NKV, C_MG)
    for j in range(6):
        blk = nkv[:, j * LANES:(j + 1) * LANES]
        if j % 2 == 0:
            blk = _rope(blk, c, s1, s2)
        if j < 4:
            nsa_ref[:, j * LANES:(j + 1) * LANES] = blk
        else:
            win_ref[:, (j - 4) * LANES:(j - 3) * LANES] = blk

    for j in range(3):
        a = C_MG + j * 1024
        mg_ref[:, j * 1024:(j + 1) * 1024] = _sigmoid(seg(a, a + 1024)).astype(BF16)

    zs = seg(C_SM, C_END)
    lane = lax.broadcasted_iota(jnp.int32, zs.shape, 1)
    logsig = jnp.minimum(zs, 0.0) - jnp.log(1.0 + jnp.exp(-jnp.abs(zs)))
    sm_ref[...] = jnp.where(lane < 8, logsig, _sigmoid(zs))


def _inproj(x2, sc, sh, mod_idx, g, w, b, tabs, tab_idx, tm, rg_shape, rg_idx):
    n, d = x2.shape
    nt = n // tm
    r = sc.shape[1]
    row = lambda i: (i, 0)
    const = lambda i: (0, 0)
    tab_spec = pl.BlockSpec((tm, LANES), tab_idx)
    in_specs = [pl.BlockSpec((tm, d), row),
                pl.BlockSpec((1, r, d), mod_idx),
                pl.BlockSpec((1, r, d), mod_idx),
                pl.BlockSpec((1, d), const),
                pl.BlockSpec((d, C_END), const),
                pl.BlockSpec((1, C_END), const),
                tab_spec, tab_spec, tab_spec]
    out_specs = [pl.BlockSpec((tm, 512), row),
                 pl.BlockSpec((tm, 1024), row),
                 pl.BlockSpec((tm, 512), rg_idx),
                 pl.BlockSpec((tm, 512), rg_idx),
                 pl.BlockSpec((tm, 512), row),
                 pl.BlockSpec((tm, 512), row),
                 pl.BlockSpec((tm, 256), row),
                 pl.BlockSpec((tm, 3072), row),
                 pl.BlockSpec((tm, LANES), row)]
    out_shape = [jax.ShapeDtypeStruct((n, 512), BF16),
                 jax.ShapeDtypeStruct((n, 1024), F32),
                 jax.ShapeDtypeStruct(rg_shape, F32),
                 jax.ShapeDtypeStruct(rg_shape, F32),
                 jax.ShapeDtypeStruct((n, 512), BF16),
                 jax.ShapeDtypeStruct((n, 512), F32),
                 jax.ShapeDtypeStruct((n, 256), F32),
                 jax.ShapeDtypeStruct((n, 3072), BF16),
                 jax.ShapeDtypeStruct((n, LANES), F32)]
    return pl.pallas_call(
        _inproj_kernel, grid=(nt,), in_specs=in_specs, out_specs=out_specs, out_shape=out_shape,
        compiler_params=_params(("parallel",)), name="inproj",
    )(x2, sc, sh, g, w, b, *tabs)


def _rg_kernel(x_ref, y_ref, cb_ref, h0_ref, wc_ref, bc_ref, wa_ref, ba_ref, wi_ref, bi_ref, lam_ref,
               o_ref, ht_ref, tail_s, h_s, a_s, u_s, *, tm, bb, w):
    @pl.when(pl.program_id(0) == 0)
    def _():
        tail_s[...] = cb_ref[...]
        h_s[...] = h0_ref[...]

    x = x_ref[...]
    xc = jnp.concatenate([tail_s[...], x], axis=0)
    conv = bc_ref[...] + wc_ref[0] * xc[0:tm]
    for i in range(1, 4):
        conv = conv + wc_ref[i] * xc[i:i + tm]
    tail_s[...] = xc[tm:tm + 3]

    conv2 = conv.reshape(tm * bb, w)
    xb = conv2.astype(BF16)
    r = _sigmoid(_dot(xb, wa_ref[...]) + ba_ref[...])
    ig = _sigmoid(_dot(xb, wi_ref[...]) + bi_ref[...])
    nl = -lam_ref[...]
    softplus = jnp.maximum(nl, 0.0) + jnp.log(1.0 + jnp.exp(-jnp.abs(nl)))
    log_a = (-RG_C * r) * softplus
    a_s[...] = jnp.exp(log_a).reshape(tm, bb, w)
    u_s[...] = (jnp.sqrt(1.0 - jnp.exp(2.0 * log_a)) * (ig * conv2)).reshape(tm, bb, w)

    def step(t, h):
        h = a_s[t] * h + u_s[t]
        u_s[t] = h
        return h

    h = lax.fori_loop(0, tm, step, h_s[...])
    h_s[...] = h
    ht_ref[...] = h

    y = y_ref[...]
    gelu = 0.5 * y * (1.0 + jnp.tanh(0.7978845608028654 * (y + 0.044715 * (y * y * y))))
    o_ref[...] = (u_s[...] * gelu).astype(o_ref.dtype)


def _rg(x_tm, y_tm, cb_tm, h0, wc, bc, wa, ba, wi, bi, lam, tm):
    t, bb, w = x_tm.shape
    nt = t // tm
    tile = lambda i: (i, 0, 0)
    c2 = lambda i: (0, 0)
    c3 = lambda i: (0, 0, 0)
    kern = functools.partial(_rg_kernel, tm=tm, bb=bb, w=w)
    return pl.pallas_call(
        kern, grid=(nt,),
        in_specs=[pl.BlockSpec((tm, bb, w), tile), pl.BlockSpec((tm, bb, w), tile),
                  pl.BlockSpec((3, bb, w), c3), pl.BlockSpec((bb, w), c2),
                  pl.BlockSpec((4, 1, w), c3), pl.BlockSpec((1, w), c2),
                  pl.BlockSpec((w, w), c2), pl.BlockSpec((1, w), c2),
                  pl.BlockSpec((w, w), c2), pl.BlockSpec((1, w), c2),
                  pl.BlockSpec((1, w), c2)],
        out_specs=[pl.BlockSpec((tm, bb, w), tile), pl.BlockSpec((bb, w), c2)],
        out_shape=[jax.ShapeDtypeStruct((t, bb, w), BF16), jax.ShapeDtypeStruct((bb, w), F32)],
        scratch_shapes=[pltpu.VMEM((3, bb, w), F32), pltpu.VMEM((bb, w), F32),
                        pltpu.VMEM((tm, bb, w), F32), pltpu.VMEM((tm, bb, w), F32)],
        compiler_params=_params(("arbitrary",)), name="rg_lru",
    )(x_tm, y_tm, cb_tm, h0, wc, bc, wa, ba, wi, bi, lam)


def _fox_kernel(q_ref, kv_ref, fc_ref, fr_ref, o_ref, *, tq):
    qi = pl.program_id(1)
    lane = lax.broadcasted_iota(jnp.int32, (tq, LANES), 1)
    lo = lane < HEAD_DIM
    qpos = qi * tq + lax.broadcasted_iota(jnp.int32, (tq, tq), 0)
    kcol = lax.broadcasted_iota(jnp.int32, (tq, tq), 1)
    hw = kv_ref.shape[2] // 2

    for j in range(hw // LANES):
        q2 = q_ref[0, :, j * LANES:(j + 1) * LANES].astype(F32)
        q_lo = jnp.where(lo, q2, 0.0).astype(BF16)
        q_hi = jnp.where(lo, 0.0, q2).astype(BF16)
        fq_lo = fc_ref[0, :, 2 * j:2 * j + 1]
        fq_hi = fc_ref[0, :, 2 * j + 1:2 * j + 2]

        def body(ki, carry, j=j, q_lo=q_lo, q_hi=q_hi, fq_lo=fq_lo, fq_hi=fq_hi):
            m_lo, l_lo, m_hi, l_hi, acc = carry
            r0 = pl.multiple_of(ki * tq, tq)
            k2 = kv_ref[0, pl.ds(r0, tq), j * LANES:(j + 1) * LANES].astype(BF16)
            v2 = kv_ref[0, pl.ds(r0, tq), hw + j * LANES:hw + (j + 1) * LANES].astype(BF16)
            fk = fr_ref[0, ki]
            ok = (kcol + r0) <= qpos

            def one(qh, fq, fkrow, m, l):
                s = _dot_nt(qh, k2) + fq - fkrow
                s = jnp.where(ok, s, NEG_INF)
                mn = jnp.maximum(m, jnp.max(s, axis=1, keepdims=True))
                al = jnp.exp(m - mn)
                p = jnp.exp(s - mn)
                l = al * l + jnp.sum(p, axis=1, keepdims=True)
                return mn, l, al, _dot(p.astype(BF16), v2)

            m_lo, l_lo, a_lo, pv_lo = one(q_lo, fq_lo, fk[2 * j:2 * j + 1, :], m_lo, l_lo)
            m_hi, l_hi, a_hi, pv_hi = one(q_hi, fq_hi, fk[2 * j + 1:2 * j + 2, :], m_hi, l_hi)
            acc = jnp.where(lo, a_lo * acc + pv_lo, a_hi * acc + pv_hi)
            return m_lo, l_lo, m_hi, l_hi, acc

        neg = jnp.full((tq, 1), NEG_INF, F32)
        zero = jnp.zeros((tq, 1), F32)
        init = (neg, zero, neg, zero, jnp.zeros((tq, LANES), F32))
        _, l_lo, _, l_hi, acc = lax.fori_loop(0, qi + 1, body, init)
        inv = jnp.where(lo, 1.0 / l_lo, 1.0 / l_hi)
        o_ref[0, :, j * LANES:(j + 1) * LANES] = (acc * inv).astype(o_ref.dtype)


def _fox_prompt(fq, fkv, f_cum, tq):
    b, t, wq = fq.shape
    nh = f_cum.shape[2]
    nq = t // tq
    f_row = jnp.transpose(f_cum.reshape(b, nq, tq, nh), (0, 1, 3, 2))
    kern = functools.partial(_fox_kernel, tq=tq)
    return pl.pallas_call(
        kern, grid=(b, nq),
        in_specs=[pl.BlockSpec((1, tq, wq), lambda bi, qi: (bi, qi, 0)),
                  pl.BlockSpec((1, t, 2 * wq), lambda bi, qi: (bi, 0, 0)),
                  pl.BlockSpec((1, tq, nh), lambda bi, qi: (bi, qi, 0)),
                  pl.BlockSpec((1, nq, nh, tq), lambda bi, qi: (bi, 0, 0, 0))],
        out_specs=pl.BlockSpec((1, tq, wq), lambda bi, qi: (bi, qi, 0)),
        out_shape=jax.ShapeDtypeStruct((b, t, wq), BF16),
        compiler_params=_params(("parallel", "arbitrary")), name="fox_prompt",
    )(fq, fkv, f_cum, f_row)


def _head_to_group_half(q2, half, g, lo):
    if half != g:
        q2 = pltpu.roll(q2, HEAD_DIM, 1)
    return jnp.where(lo if g == 0 else jnp.logical_not(lo), q2, 0.0)


def _nsa_cmp_kernel(q_ref, kv_ref, wk_ref, wv_ref, sm_ref, ocmp_ref, sel_ref, kc_s, vc_s, *, tq, nb, n_sel):
    qi = pl.program_id(1)

    @pl.when(qi == 0)
    def _():
        kc = kv_ref[0, :, 0:LANES].reshape(nb, NSA_BLOCK, LANES)
        vc = kv_ref[0, :, LANES:2 * LANES].reshape(nb, NSA_BLOCK, LANES)
        kc_s[...] = jnp.zeros(kc_s.shape, F32)
        vc_s[...] = jnp.zeros(vc_s.shape, F32)
        kc_s[0:nb, :] = jnp.sum(kc * wk_ref[...], axis=1)
        vc_s[0:nb, :] = jnp.sum(vc * wv_ref[...], axis=1)

    pos = qi * tq + lax.broadcasted_iota(jnp.int32, (tq, LANES), 0)
    n = lax.broadcasted_iota(jnp.int32, (tq, LANES), 1)
    lo = n < HEAD_DIM
    complete = jnp.logical_and((n + 1) * NSA_BLOCK - 1 <= pos, n < nb)
    any_c = jnp.where(pos >= NSA_BLOCK - 1, 1.0, 0.0)
    kc_b = kc_s[...].astype(BF16)
    vc_b = vc_s[...].astype(BF16)

    imp = [jnp.zeros((tq, LANES), F32), jnp.zeros((tq, LANES), F32)]
    outs = []
    for h in range(2 * NSA_HPG):
        g, half, blk = h // NSA_HPG, h % 2, h // 2
        q2 = q_ref[0, :, blk * LANES:(blk + 1) * LANES].astype(F32)
        qh = _head_to_group_half(q2, half, g, lo).astype(BF16)
        s = jnp.where(complete, _dot_nt(qh, kc_b), NEG_INF)
        e = jnp.exp(s - jnp.max(s, axis=1, keepdims=True))
        p = e / jnp.sum(e, axis=1, keepdims=True) * any_c
        imp[g] = imp[g] + p
        o = _dot(p.astype(BF16), vc_b) * sm_ref[0, :, 8 + 3 * h:9 + 3 * h]
        if half != g:
            o = pltpu.roll(o, HEAD_DIM, 1)
        outs.append(o)
    for blk in range(NSA_HPG):
        ocmp_ref[0, :, blk * LANES:(blk + 1) * LANES] = jnp.where(lo, outs[2 * blk], outs[2 * blk + 1])

    cur = pos // NSA_BLOCK
    forced = jnp.logical_or(n == cur, n == 0)
    nf = n.astype(F32)
    for g in range(2):
        score = jnp.where(forced, 2.0 * NSA_HPG, jnp.where(n <= cur, imp[g], -1.0))
        score = jnp.where(n < nb, score, -2.0)
        sel = jnp.zeros((tq, LANES), F32)
        for _ in range(n_sel):
            mx = jnp.max(score, axis=1, keepdims=True)
            first = jnp.min(jnp.where(score == mx, nf, float(LANES)), axis=1, keepdims=True)
            pick = nf == first
            sel = jnp.where(pick, 1.0, sel)
            score = jnp.where(pick, -3.0, score)
        sel_ref[0, :, g * LANES:(g + 1) * LANES] = sel


def _nsa_cmp(nq, nsa, wk, wv, sm, tq):
    b, t, wq = nq.shape
    nb = t // NSA_BLOCK
    kern = functools.partial(_nsa_cmp_kernel, tq=tq, nb=nb, n_sel=min(NSA_TOPK, nb))
    tile = lambda bi, qi: (bi, qi, 0)
    return pl.pallas_call(
        kern, grid=(b, t // tq),
        in_specs=[pl.BlockSpec((1, tq, wq), tile),
                  pl.BlockSpec((1, t, nsa.shape[2]), lambda bi, qi: (bi, 0, 0)),
                  pl.BlockSpec((NSA_BLOCK, LANES), lambda bi, qi: (0, 0)),
                  pl.BlockSpec((NSA_BLOCK, LANES), lambda bi, qi: (0, 0)),
                  pl.BlockSpec((1, tq, LANES), tile)],
        out_specs=[pl.BlockSpec((1, tq, wq), tile), pl.BlockSpec((1, tq, 2 * LANES), tile)],
        out_shape=[jax.ShapeDtypeStruct((b, t, wq), F32), jax.ShapeDtypeStruct((b, t, 2 * LANES), F32)],
        scratch_shapes=[pltpu.VMEM((LANES, LANES), F32), pltpu.VMEM((LANES, LANES), F32)],
        compiler_params=_params(("parallel", "arbitrary")), name="nsa_cmp_select",
    )(nq, nsa, wk, wv, sm)


def _nsa_sw_kernel(q_ref, kv_ref, win_ref, sel_ref, ocmp_ref, sm_ref, o_ref, *, tq):
    qi = pl.program_id(1)
    lane = lax.broadcasted_iota(jnp.int32, (tq, LANES), 1)
    lo = lane < HEAD_DIM
    qpos = qi * tq + lax.broadcasted_iota(jnp.int32, (tq, tq), 0)
    kcol = lax.broadcasted_iota(jnp.int32, (tq, tq), 1)
    blk_row = lax.broadcasted_iota(jnp.int32, (LANES, tq), 0)
    blk_col = lax.broadcasted_iota(jnp.int32, (LANES, tq), 1)
    rows = NSA_HPG * tq

    def attend(q4, k2, v2, okf, carry):
        m, l, acc = carry
        ok4 = jnp.concatenate([okf] * NSA_HPG, axis=0)
        s = jnp.where(ok4 > 0.5, _dot_nt(q4, k2), NEG_INF)
        mn = jnp.maximum(m, jnp.max(s, axis=1, keepdims=True))
        al = jnp.exp(m - mn)
        p = jnp.exp(s - mn)
        l = al * l + jnp.sum(p, axis=1, keepdims=True)
        acc = al * acc + _dot(p.astype(BF16), v2)
        return mn, l, acc

    init = (jnp.full((rows, 1), NEG_INF, F32), jnp.zeros((rows, 1), F32), jnp.zeros((rows, LANES), F32))
    outs = [None] * (2 * NSA_HPG)
    for g in range(2):
        qs = []
        for hl in range(NSA_HPG):
            h = NSA_HPG * g + hl
            q2 = q_ref[0, :, (h // 2) * LANES:(h // 2 + 1) * LANES].astype(F32)
            qs.append(_head_to_group_half(q2, h % 2, g, lo).astype(BF16))
        q4 = jnp.concatenate(qs, axis=0)
        selg = sel_ref[0, :, g * LANES:(g + 1) * LANES].astype(BF16)

        def sel_body(ki, carry, q4=q4, selg=selg):
            r0 = pl.multiple_of(ki * tq, tq)
            k2 = kv_ref[0, pl.ds(r0, tq), 2 * LANES:3 * LANES].astype(BF16)
            v2 = kv_ref[0, pl.ds(r0, tq), 3 * LANES:4 * LANES].astype(BF16)
            expand = jnp.where((blk_col + r0) // NSA_BLOCK == blk_row, 1.0, 0.0).astype(BF16)
            chosen = _dot(selg, expand)
            okf = jnp.where((kcol + r0) <= qpos, chosen, 0.0)
            return attend(q4, k2, v2, okf, carry)

        _, l_s, acc_s = lax.fori_loop(0, qi + 1, sel_body, init)

        def win_body(ki, carry, q4=q4):
            r0 = pl.multiple_of(ki * tq, tq)
            k2 = win_ref[0, pl.ds(r0, tq), 0:LANES].astype(BF16)
            v2 = win_ref[0, pl.ds(r0, tq), LANES:2 * LANES].astype(BF16)
            diff = qpos - (kcol + r0)
            okf = jnp.where(jnp.logical_and(diff >= 0, diff < NSA_WINDOW), 1.0, 0.0)
            return attend(q4, k2, v2, okf, carry)

        first = jnp.maximum(qi * tq - (NSA_WINDOW - 1), 0) // tq
        _, l_w, acc_w = lax.fori_loop(first, qi + 1, win_body, init)

        o_s = acc_s / l_s
        o_w = acc_w / l_w
        for hl in range(NSA_HPG):
            h = NSA_HPG * g + hl
            g_sel = sm_ref[0, :, 9 + 3 * h:10 + 3 * h]
            g_win = sm_ref[0, :, 10 + 3 * h:11 + 3 * h]
            o = g_sel * o_s[hl * tq:(hl + 1) * tq] + g_win * o_w[hl * tq:(hl + 1) * tq]
            if h % 2 != g:
                o = pltpu.roll(o, HEAD_DIM, 1)
            outs[h] = o
    for blk in range(NSA_HPG):
        merged = jnp.where(lo, outs[2 * blk], outs[2 * blk + 1])
        o_ref[0, :, blk * LANES:(blk + 1) * LANES] = (
            merged + ocmp_ref[0, :, blk * LANES:(blk + 1) * LANES]).astype(o_ref.dtype)


def _nsa_sw(nq, nsa, win, sel, ocmp, sm, tq):
    b, t, wq = nq.shape
    tile = lambda bi, qi: (bi, qi, 0)
    full = lambda bi, qi: (bi, 0, 0)
    kern = functools.partial(_nsa_sw_kernel, tq=tq)
    return pl.pallas_call(
        kern, grid=(b, t // tq),
        in_specs=[pl.BlockSpec((1, tq, wq), tile),
                  pl.BlockSpec((1, t, nsa.shape[2]), full),
                  pl.BlockSpec((1, t, win.shape[2]), full),
                  pl.BlockSpec((1, tq, 2 * LANES), tile),
                  pl.BlockSpec((1, tq, wq), tile),
                  pl.BlockSpec((1, tq, LANES), tile)],
        out_specs=pl.BlockSpec((1, tq, wq), tile),
        out_shape=jax.ShapeDtypeStruct((b, t, wq), BF16),
        compiler_params=_params(("parallel", "arbitrary")), name="nsa_sel_win",
    )(nq, nsa, win, sel, ocmp, sm)


def _merge_kernel(x_ref, of_ref, or_ref, on_ref, mg_ref, gt_ref, wf_ref, wr_ref, wn_ref, wo_ref, xo_ref):
    d = x_ref.shape[1]
    mixed = (mg_ref[:, 0:d].astype(F32) * _dot(of_ref[...], wf_ref[...])
             + mg_ref[:, d:2 * d].astype(F32) * _dot(or_ref[...], wr_ref[...])
             + mg_ref[:, 2 * d:3 * d].astype(F32) * _dot(on_ref[...], wn_ref[...]))
    out = _dot(mixed.astype(BF16), wo_ref[...])
    xo_ref[...] = x_ref[...] + gt_ref[0] * out


def _merge(x2, o_fox, o_rg, rg_idx, o_nsa, mg, gt, mod_idx, wf, wr, wn, wo, tm):
    n, d = x2.shape
    r = gt.shape[1]
    row = lambda i: (i, 0)
    const = lambda i: (0, 0)
    wb = wf.shape[0]
    return pl.pallas_call(
        _merge_kernel, grid=(n // tm,),
        in_specs=[pl.BlockSpec((tm, d), row), pl.BlockSpec((tm, wb), row), pl.BlockSpec((tm, wb), rg_idx),
                  pl.BlockSpec((tm, wb), row), pl.BlockSpec((tm, 3 * d), row), pl.BlockSpec((1, r, d), mod_idx),
                  pl.BlockSpec((wb, d), const), pl.BlockSpec((wb, d), const), pl.BlockSpec((wb, d), const),
                  pl.BlockSpec((d, d), const)],
        out_specs=pl.BlockSpec((tm, d), row),
        out_shape=jax.ShapeDtypeStruct((n, d), F32),
        compiler_params=_params(("parallel",)), name="merge",
    )(x2, o_fox, o_rg, o_nsa, mg, gt, wf, wr, wn, wo)


def _mlp_kernel(x_ref, g_ref, sc_ref, sh_ref, gt_ref, wu_ref, wd_ref, gf_ref, xo_ref, *, final):
    x = x_ref[...]
    u = _rms_mod(x, g_ref[...], sc_ref[0], sh_ref[0])
    hdn = jnp.square(jnp.maximum(_dot(u.astype(BF16), wu_ref[...]), 0.0))
    xn = x + gt_ref[0] * _dot(hdn.astype(BF16), wd_ref[...])
    if final:
        ms = jnp.mean(xn * xn, axis=-1, keepdims=True)
        xn = xn * lax.rsqrt(ms + NORM_EPS) * gf_ref[...]
    xo_ref[...] = xn


def _mlp(x2, g, sc, sh, gt, mod_idx, wu, wd, gf, final, tm):
    n, d = x2.shape
    r = sc.shape[1]
    dff = wu.shape[1]
    row = lambda i: (i, 0)
    const = lambda i: (0, 0)
    mod = pl.BlockSpec((1, r, d), mod_idx)
    return pl.pallas_call(
        functools.partial(_mlp_kernel, final=final), grid=(n // tm,),
        in_specs=[pl.BlockSpec((tm, d), row), pl.BlockSpec((1, d), const), mod, mod, mod,
                  pl.BlockSpec((d, dff), const), pl.BlockSpec((dff, d), const), pl.BlockSpec((1, d), const)],
        out_specs=pl.BlockSpec((tm, d), row),
        out_shape=jax.ShapeDtypeStruct((n, d), F32),
        compiler_params=_params(("parallel",)), name="mlp",
    )(x2, g, sc, sh, gt, wu, wd, gf)


def _softmax(s):
    e = jnp.exp(s - jnp.max(s, axis=-1, keepdims=True))
    return e / jnp.sum(e, axis=-1, keepdims=True)


def _fox_sample(fq, fkv, logf, cache_kv, cache_logf, page_table, past_len):
    db, dt = page_table.shape[0], fq.shape[0] // page_table.shape[0]
    nh = logf.shape[-1]
    past = cache_kv[page_table].reshape(db, past_len, 2, nh, HEAD_DIM)
    new = fkv.reshape(db, dt, 2, nh, HEAD_DIM)
    k_all = jnp.concatenate([past[:, :, 0], new[:, :, 0]], axis=1).astype(BF16)
    v_all = jnp.concatenate([past[:, :, 1], new[:, :, 1]], axis=1).astype(BF16)
    lf = jnp.concatenate([cache_logf[page_table].reshape(db, past_len, nh), logf.reshape(db, dt, nh)], axis=1)
    f_cum = jnp.cumsum(lf, axis=1)
    q = fq.reshape(db, dt, nh, HEAD_DIM)
    s = jnp.einsum('bthd,bshd->bhts', q, k_all, preferred_element_type=F32)
    ft = jnp.transpose(f_cum, (0, 2, 1))
    s = s + ft[:, :, past_len:, None] - ft[:, :, None, :]
    qp = past_len + jnp.arange(dt)
    s = jnp.where(jnp.arange(past_len + dt)[None, :] <= qp[:, None], s, NEG_INF)
    p = _softmax(s)
    o = jnp.einsum('bhts,bshd->bthd', p.astype(BF16), v_all, preferred_element_type=F32)
    return o.reshape(db * dt, nh * HEAD_DIM).astype(BF16)


def _nsa_sample(nq, nsa_new, win_new, sm, cache_nsa, state_win, page_table, pool, past_len):
    db = page_table.shape[0]
    dt = nq.shape[0] // db
    g_, hpg = 2, NSA_HPG
    full = jnp.concatenate([cache_nsa[page_table].reshape(db, past_len, 4, g_, HEAD_DIM),
                            nsa_new.reshape(db, dt, 4, g_, HEAD_DIM)], axis=1)
    win_all = jnp.concatenate([state_win, win_new.reshape(db, dt, 2, g_, HEAD_DIM)], axis=1)
    keep = state_win.shape[1]
    ln = past_len + dt
    nb = -(-ln // NSA_BLOCK)
    pad = nb * NSA_BLOCK - ln
    q_pos = past_len + jnp.arange(dt)
    qg = nq.reshape(db, dt, g_, hpg, HEAD_DIM)
    gates = sm[:, 8:8 + 3 * g_ * hpg].reshape(db, dt, g_, hpg, 3)

    def to_blocks(a):
        return jnp.pad(a, ((0, 0), (0, pad), (0, 0), (0, 0))).reshape(db, nb, NSA_BLOCK, g_, HEAD_DIM)

    k_cmp = jnp.einsum('bnlgd,gl->bngd', to_blocks(full[:, :, 0]), pool[0], precision='highest')
    v_cmp = jnp.einsum('bnlgd,gl->bngd', to_blocks(full[:, :, 1]), pool[1], precision='highest')
    blk = jnp.arange(nb)
    cur = q_pos // NSA_BLOCK
    complete = ((blk[None, :] + 1) * NSA_BLOCK - 1) <= q_pos[:, None]
    s = jnp.einsum('btghd,bngd->bghtn', qg, k_cmp.astype(BF16), preferred_element_type=F32)
    s = jnp.where(complete, s, NEG_INF)
    any_c = jnp.any(complete, axis=-1).astype(F32)[None, None, None, :, None]
    p_cmp = _softmax(s) * any_c
    o_cmp = jnp.einsum('bghtn,bngd->btghd', p_cmp.astype(BF16), v_cmp.astype(BF16), preferred_element_type=F32)
    imp = jnp.sum(p_cmp, axis=2)
    forced = (blk[None, :] == cur[:, None]) | (blk[None, :] == 0)
    causal_blk = blk[None, :] <= cur[:, None]
    score = jnp.where(forced, 2.0 * hpg, jnp.where(causal_blk, imp, -1.0))
    _, idx = lax.top_k(score, min(NSA_TOPK, nb))
    chosen = jnp.any(idx[..., None] == blk, axis=-2)
    kpos = jnp.arange(nb * NSA_BLOCK)
    ok = jnp.repeat(chosen, NSA_BLOCK, axis=-1) & (kpos[None, :] <= q_pos[:, None])
    ks = jnp.pad(full[:, :, 2], ((0, 0), (0, pad), (0, 0), (0, 0))).astype(BF16)
    vs = jnp.pad(full[:, :, 3], ((0, 0), (0, pad), (0, 0), (0, 0))).astype(BF16)
    ss = jnp.einsum('btghd,bsgd->bghts', qg, ks, preferred_element_type=F32)
    ps = _softmax(jnp.where(ok[:, :, None], ss, NEG_INF))
    o_sel = jnp.einsum('bghts,bsgd->btghd', ps.astype(BF16), vs, preferred_element_type=F32)
    win_pos = past_len - keep + jnp.arange(keep + dt)
    diff = q_pos[:, None] - win_pos[None, :]
    m = (diff >= 0) & (diff < NSA_WINDOW)
    sw = jnp.einsum('btghd,bsgd->bghts', qg, win_all[:, :, 0].astype(BF16), preferred_element_type=F32)
    pw = _softmax(jnp.where(m, sw, NEG_INF))
    o_win = jnp.einsum('bghts,bsgd->btghd', pw.astype(BF16), win_all[:, :, 1].astype(BF16),
                       preferred_element_type=F32)
    o = gates[..., 0:1] * o_cmp + gates[..., 1:2] * o_sel + gates[..., 2:3] * o_win
    return o.reshape(db * dt, g_ * hpg * HEAD_DIM).astype(BF16), win_all[:, win_all.shape[1] - keep:]


def _rope_tables(pos):
    inv = ROPE_THETA ** (-jnp.arange(ROT_HALF, dtype=F32) / ROT_HALF)
    ang = pos.astype(F32)[:, None] * inv[None, :]
    cos, sin = jnp.cos(ang), jnp.sin(ang)
    n = pos.shape[0]
    one = jnp.ones((n, HEAD_DIM - 2 * ROT_HALF), F32)
    zero = jnp.zeros((n, HEAD_DIM - 2 * ROT_HALF), F32)
    z8 = jnp.zeros((n, ROT_HALF), F32)
    c = jnp.concatenate([cos, cos, one], axis=1)
    s1 = jnp.concatenate([-sin, z8, zero], axis=1)
    s2 = jnp.concatenate([z8, sin, zero], axis=1)
    return tuple(jnp.concatenate([a, a], axis=1) for a in (c, s1, s2))


def _block_diag(w):
    nblk, bw, _ = w.shape
    eye = jnp.eye(nblk, dtype=w.dtype)
    return jnp.einsum('nij,nm->nimj', w, eye).reshape(nblk * bw, nblk * bw)


def kernel(x_prompt, x_sample, cache_fox_kv, cache_fox_logf, cache_nsa_kv, state_win_kv, state_rg_h,
           state_rg_conv, page_table, c_prompt, c_sample, norm_mix, norm_mlp, norm_final, w_ada, b_ada,
           w_in, b_in, rg_w_conv, rg_b_conv, rg_w_a, rg_b_a, rg_w_i, rg_b_i, rg_lam, nsa_pool,
           w_br_fox, w_br_rg, w_br_nsa, w_out, w_up, w_down):
    b, t, d = x_prompt.shape
    db, dt, _ = x_sample.shape
    depth = w_in.shape[0]
    past_len = page_table.shape[1] * cache_fox_kv.shape[2]
    nh = cache_fox_logf.shape[-1]
    rgw = rg_lam.shape[1]

    def reorder(a):
        pad = jnp.zeros(a.shape[:-1] + (C_END - C_SM - 32,), a.dtype)
        return jnp.concatenate([a[..., 0:1536], a[..., 1544:3848], a[..., 3872:6944],
                                a[..., 1536:1544], a[..., 3848:3872], pad], axis=-1)

    w_in_r = reorder(w_in).astype(BF16)
    b_in_r = reorder(b_in)[:, None, :]
    w_ada_b = w_ada.astype(BF16)
    wf_b, wr_b, wn_b = w_br_fox.astype(BF16), w_br_rg.astype(BF16), w_br_nsa.astype(BF16)
    wo_b, wu_b, wd_b = w_out.astype(BF16), w_up.astype(BF16), w_down.astype(BF16)
    wa_bd = jax.vmap(_block_diag)(rg_w_a).astype(BF16)
    wi_bd = jax.vmap(_block_diag)(rg_w_i).astype(BF16)
    pool_w = jnp.repeat(jnp.transpose(nsa_pool, (0, 1, 3, 2)), HEAD_DIM, axis=-1)

    m_all = _ada(jnp.concatenate([c_prompt, c_sample], axis=0), w_ada_b, b_ada)

    def run_group(x, rows, pos0, is_prompt):
        bsz, tlen, _ = x.shape
        n = bsz * tlen
        x2 = x.reshape(n, d)
        if is_prompt:
            tm = min(256, tlen)
            tpb = tlen // tm
            mod_idx = lambda i: (i // tpb, 0, 0)
            expand = lambda a: a[:, None, :]
            tabs = _rope_tables(pos0 + jnp.arange(tlen))
            tab_idx = lambda i: (i % tpb, 0)
            rg_shape = (tlen, bsz * rgw)
            rg_idx = lambda i: (i % tpb, i // tpb)
        else:
            tm = min(256, n)
            mod_idx = lambda i: (i, 0, 0)
            expand = lambda a: jnp.repeat(a, tlen, axis=0).reshape(n // tm, tm, d)
            tabs = _rope_tables(jnp.tile(pos0 + jnp.arange(tlen), bsz))
            tab_idx = lambda i: (i, 0)
            rg_shape = (n, rgw)
            rg_idx = lambda i: (i, 0)

        states = []
        for l in range(depth):
            m = m_all[l, rows[0]:rows[1]]
            sh1, sc1, gt1, sh2, sc2, gt2 = [expand(a) for a in jnp.split(m, 6, axis=-1)]
            fq, fkv, rx, ry, nq, nsa, win, mg, sm = _inproj(
                x2, sc1, sh1, mod_idx, norm_mix[l][None, :], w_in_r[l], b_in_r[l], tabs, tab_idx, tm,
                rg_shape, rg_idx)
            logf = sm[:, 0:nh]

            if is_prompt:
                x_tm = rx.reshape(tlen, bsz, rgw)
                y_tm = ry.reshape(tlen, bsz, rgw)
                cb_tm = jnp.zeros((3, bsz, rgw), F32)
                h0 = jnp.zeros((bsz, rgw), F32)
                rg_tm = min(256, tlen)
            else:
                x_tm = jnp.transpose(rx.reshape(bsz, tlen, rgw), (1, 0, 2))
                y_tm = jnp.transpose(ry.reshape(bsz, tlen, rgw), (1, 0, 2))
                cb_tm = jnp.transpose(state_rg_conv[l], (1, 0, 2))
                h0 = state_rg_h[l]
                rg_tm = tlen
            o_rg_tm, h_t = _rg(x_tm, y_tm, cb_tm, h0, rg_w_conv[l][:, None, :], rg_b_conv[l][None, :],
                               wa_bd[l], rg_b_a[l][None, :], wi_bd[l], rg_b_i[l][None, :],
                               rg_lam[l][None, :], rg_tm)
            xc_tail = jnp.concatenate([cb_tm, x_tm[-3:]], axis=0)[-3:]
            new_buf = jnp.transpose(xc_tail, (1, 0, 2))
            if is_prompt:
                o_rg = o_rg_tm.reshape(tlen, bsz * rgw)
            else:
                o_rg = jnp.transpose(o_rg_tm, (1, 0, 2)).reshape(n, rgw)

            if is_prompt:
                f_cum = jnp.cumsum(logf.reshape(bsz, tlen, nh), axis=1)
                o_fox = _fox_prompt(fq.reshape(bsz, tlen, -1), fkv.reshape(bsz, tlen, -1), f_cum,
                                    min(256, tlen)).reshape(n, -1)
                tq = min(128, tlen)
                nq3, nsa3 = nq.reshape(bsz, tlen, -1), nsa.reshape(bsz, tlen, -1)
                sm3 = sm.reshape(bsz, tlen, LANES)
                ocmp, sel = _nsa_cmp(nq3, nsa3, pool_w[l, 0], pool_w[l, 1], sm3, tq)
                o_nsa = _nsa_sw(nq3, nsa3, win.reshape(bsz, tlen, -1), sel, ocmp, sm3, tq).reshape(n, -1)
                keep = min(NSA_WINDOW, tlen)
                new_win = win.reshape(bsz, tlen, 2, 2, HEAD_DIM)[:, tlen - keep:]
            else:
                o_fox = _fox_sample(fq, fkv, logf, cache_fox_kv[l], cache_fox_logf[l], page_table, past_len)
                o_nsa, new_win = _nsa_sample(nq, nsa, win, sm, cache_nsa_kv[l], state_win_kv[l], page_table,
                                             nsa_pool[l], past_len)

            x2 = _merge(x2, o_fox, o_rg, rg_idx, o_nsa, mg, gt1, mod_idx, wf_b[l], wr_b[l], wn_b[l], wo_b[l], tm)
            x2 = _mlp(x2, norm_mlp[l][None, :], sc2, sh2, gt2, mod_idx, wu_b[l], wd_b[l], norm_final[None, :],
                      l == depth - 1, tm)
            states.append((fkv.reshape(bsz, tlen, 2, nh, HEAD_DIM), logf.reshape(bsz, tlen, nh),
                           nsa.reshape(bsz, tlen, 4, 2, HEAD_DIM), new_win, h_t, new_buf))
        stacked = [jnp.stack(list(s), axis=0) for s in zip(*states)]
        return x2.reshape(bsz, tlen, d), stacked

    y_p, p_states = run_group(x_prompt, (0, b), 0, True)
    y_s, s_states = run_group(x_sample, (b, b + db), past_len, False)
    return (y_p, y_s, *p_states, *s_states)
```
